```python
import jax, jax.numpy as jnp
from jax import lax
import numpy as np

D_MODEL = 1024
BATCH = 8
SEQ = 2048
DEPTH = 4

HEAD_DIM = 64
DIL_PAIRS = ((128, 1), (512, 4), (2048, 16))
N_DIL_GROUPS = len(DIL_PAIRS)
DIL_HEADS_PER_GROUP = 4
DIL_HEADS = N_DIL_GROUPS * DIL_HEADS_PER_GROUP
SB_HEADS = 4
DIL_WIDTH = DIL_HEADS * HEAD_DIM
SB_WIDTH = SB_HEADS * HEAD_DIM
DIL_OUT_WIDTH = DIL_HEADS_PER_GROUP * HEAD_DIM
N_BRANCHES = 2
IN_COLS = 3 * DIL_WIDTH + 3 * SB_WIDTH + N_BRANCHES * D_MODEL
D_FF = 2816
CONV_WIDTH = 3
ROPE_THETA = 500000.0
ROPE_DIM = HEAD_DIM // 4
Q_BLOCK = 128
NORM_EPS = 1e-5
MAX_POS_OFFSET = 4096
MASK_VALUE = -1e30

kernel_name = "hybrid_dilated_stickbreaking_convffn"


def rms_norm(x, gain):
    xf = x.astype(jnp.float32)
    xf = xf * lax.rsqrt(jnp.mean(xf * xf, axis=-1, keepdims=True) + NORM_EPS)
    return (xf * gain.astype(jnp.float32)).astype(x.dtype)


def rope_tables(positions, dtype):
    inv_freq = ROPE_THETA ** (-jnp.arange(0, ROPE_DIM, 2, dtype=jnp.float32) / ROPE_DIM)
    ang = positions.astype(jnp.float32)[..., None] * inv_freq
    return (jnp.cos(ang)[:, :, None, None, :].astype(dtype),
            jnp.sin(ang)[:, :, None, None, :].astype(dtype))


def apply_partial_rope(x, cos, sin):
    half = ROPE_DIM // 2
    x1, x2, rest = x[..., :half], x[..., half:ROPE_DIM], x[..., ROPE_DIM:]
    return jnp.concatenate([x1 * cos - x2 * sin, x2 * cos + x1 * sin, rest], axis=-1)


def dilated_attention(q, k, v):
    b, s = q.shape[0], q.shape[1]
    n_blocks = s // Q_BLOCK
    scale = HEAD_DIM ** -0.5
    qs = [q[:, :, g] for g in range(N_DIL_GROUPS)]
    ks = [k[:, :, g] for g in range(N_DIL_GROUPS)]
    vs = [v[:, :, g] for g in range(N_DIL_GROUPS)]

    def block(t0):
        t = t0 + jnp.arange(Q_BLOCK)
        outs, lses = [], []
        for g, (window, dil) in enumerate(DIL_PAIRS):
            n_keys = window // dil + 1
            idx = t[:, None] - dil * jnp.arange(n_keys)[None, :]
            valid = idx >= 0
            idx = jnp.maximum(idx, 0)
            qg = lax.dynamic_slice_in_dim(qs[g], t0, Q_BLOCK, axis=1)
            kg = ks[g][:, idx]
            vg = vs[g][:, idx]
            sc = jnp.einsum('bqhd,bqkhd->bhqk', qg, kg).astype(jnp.float32) * scale
            sc = jnp.where(valid[None, None], sc, MASK_VALUE)
            m = jnp.max(sc, axis=-1, keepdims=True)
            p = jnp.exp(sc - m)
            den = jnp.sum(p, axis=-1)
            o = jnp.einsum('bhqk,bqkhd->bqhd', p, vg.astype(jnp.float32))
            o = o / jnp.transpose(den, (0, 2, 1))[..., None]
            outs.append(o)
            lses.append(m[..., 0] + jnp.log(den))
        o_all = jnp.stack(outs, axis=0)
        w = jax.nn.softmax(jnp.stack(lses, axis=0), axis=0)
        w = jnp.transpose(w, (0, 1, 3, 2))[..., None]
        return jnp.sum(w * o_all, axis=0).astype(q.dtype)

    out = lax.map(block, jnp.arange(n_blocks) * Q_BLOCK)
    return jnp.moveaxis(out, 0, 1).reshape(b, s, DIL_OUT_WIDTH)


def stick_breaking_attention(q, k, v):
    b, s = q.shape[0], q.shape[1]
    n_blocks = s // Q_BLOCK
    scale = HEAD_DIM ** -0.5
    key_pos = jnp.arange(s)

    def block(t0):
        qb = lax.dynamic_slice_in_dim(q, t0, Q_BLOCK, axis=1)
        z = jnp.einsum('bqhd,bkhd->bhqk', qb, k).astype(jnp.float32) * scale
        t = t0 + jnp.arange(Q_BLOCK)
        causal = key_pos[None, :] < t[:, None]
        log_keep = jnp.where(causal, jax.nn.log_sigmoid(-z), 0.0)
        suffix = lax.cumsum(log_keep, axis=3, reverse=True) - log_keep
        attn = jnp.where(causal, jnp.exp(jax.nn.log_sigmoid(z) + suffix), 0.0)
        return jnp.einsum('bhqk,bkhd->bqhd', attn, v.astype(jnp.float32)).astype(q.dtype)

    out = lax.map(block, jnp.arange(n_blocks) * Q_BLOCK)
    return jnp.moveaxis(out, 0, 1).reshape(b, s, SB_WIDTH)


def conv_ffn(h, w_up, conv_w, conv_b, w_down):
    up = h @ w_up
    a, bval = up[..., :D_FF], up[..., D_FF:]
    a = lax.conv_general_dilated(a, conv_w[:, None, :].astype(a.dtype), window_strides=(1,),
                                 padding=((CONV_WIDTH - 1, 0),),
                                 dimension_numbers=('NWC', 'WIO', 'NWC'),
                                 feature_group_count=D_FF) + conv_b
    return (jax.nn.silu(a) * bval) @ w_down


def setup_inputs(seed: int = 0) -> dict:
    key = jax.random.key(seed)
    ks = jax.random.split(key, 16)
    f32 = jnp.float32
    nrm = lambda k, shape, s: jax.random.normal(k, shape, f32) * s
    x = jax.random.normal(ks[0], (BATCH, SEQ, D_MODEL), f32)
    offset = jax.random.randint(ks[1], (BATCH, 1), 0, MAX_POS_OFFSET, dtype=jnp.int32)
    positions = offset + jnp.arange(SEQ, dtype=jnp.int32)[None, :]
    return {
        "x": x,
        "positions": positions,
        "norm_mix": 1.0 + nrm(ks[2], (DEPTH, D_MODEL), 0.05),
        "w_in": nrm(ks[3], (DEPTH, D_MODEL, IN_COLS), D_MODEL ** -0.5),
        "b_gate": nrm(ks[4], (DEPTH, N_BRANCHES * D_MODEL), 0.1),
        "w_proj_a": nrm(ks[5], (DEPTH, DIL_OUT_WIDTH, D_MODEL), DIL_OUT_WIDTH ** -0.5),
        "w_proj_b": nrm(ks[6], (DEPTH, SB_WIDTH, D_MODEL), SB_WIDTH ** -0.5),
        "w_out": nrm(ks[7], (DEPTH, D_MODEL, D_MODEL), D_MODEL ** -0.5),
        "norm_ffn": 1.0 + nrm(ks[8], (DEPTH, D_MODEL), 0.05),
        "w_up": nrm(ks[9], (DEPTH, D_MODEL, 2 * D_FF), D_MODEL ** -0.5),
        "conv_w": nrm(ks[10], (DEPTH, CONV_WIDTH, D_FF), CONV_WIDTH ** -0.5),
        "conv_b": nrm(ks[11], (DEPTH, D_FF), 0.02),
        "w_down": nrm(ks[12], (DEPTH, D_FF, D_MODEL), D_FF ** -0.5),
        "norm_final": 1.0 + nrm(ks[13], (D_MODEL,), 0.05),
    }


def reference(x, positions, norm_mix, w_in, b_gate, w_proj_a, w_proj_b, w_out,
              norm_ffn, w_up, conv_w, conv_b, w_down, norm_final):
    b, s, _ = x.shape
    cos, sin = rope_tables(positions, x.dtype)
    splits = np.cumsum([DIL_WIDTH, DIL_WIDTH, DIL_WIDTH, SB_WIDTH, SB_WIDTH, SB_WIDTH]).tolist()
    for l in range(DEPTH):
        h = rms_norm(x, norm_mix[l])
        proj = h @ w_in[l]
        dq, dk, dv, sq, sk, sv, gate_logits = jnp.split(proj, splits, axis=-1)
        dshape = (b, s, N_DIL_GROUPS, DIL_HEADS_PER_GROUP, HEAD_DIM)
        dq = apply_partial_rope(dq.reshape(dshape), cos, sin)
        dk = apply_partial_rope(dk.reshape(dshape), cos, sin)
        dv = dv.reshape(dshape)
        sshape = (b, s, SB_HEADS, HEAD_DIM)
        y_a = dilated_attention(dq, dk, dv) @ w_proj_a[l]
        y_b = stick_breaking_attention(sq.reshape(sshape), sk.reshape(sshape), sv.reshape(sshape)) @ w_proj_b[l]
        gates = jax.nn.sigmoid(gate_logits + b_gate[l])
        mixed = gates[..., :D_MODEL] * y_a + gates[..., D_MODEL:] * y_b
        x = x + mixed @ w_out[l]
        h = rms_norm(x, norm_ffn[l])
        x = x + conv_ffn(h, w_up[l], conv_w[l], conv_b[l], w_down[l])
    return rms_norm(x, norm_final)
```

```python
import functools

import jax
import jax.numpy as jnp
from jax import lax
from jax.experimental import pallas as pl
from jax.experimental.pallas import tpu as pltpu

F32 = jnp.float32
BF16 = jnp.bfloat16

HEAD_DIM = 64
HEADS = 4
GROUP_WIDTH = HEADS * HEAD_DIM
QKV_WIDTH = 3 * GROUP_WIDTH
DILATIONS = (1, 4, 16)
WINDOW_KEYS = 128
N_GROUPS = len(DILATIONS)
N_SLABS = N_GROUPS + 1
ROPE_DIM = HEAD_DIM // 4
ROPE_HALF = ROPE_DIM // 2
ROPE_THETA = 500000.0
NORM_EPS = 1e-5
MASK_VALUE = -1e30
CONV_WIDTH = 3
SCALE = HEAD_DIM ** -0.5

LANES = 128
SUBLANES = 8
COL_CHUNK = 256
VMEM_LIMIT_BYTES = 56 * 1024 * 1024

TOKEN_TILE = 512
SB_BLOCK = 256


def _params(n_axes):
    return pltpu.CompilerParams(
        dimension_semantics=("arbitrary",) * n_axes,
        vmem_limit_bytes=VMEM_LIMIT_BYTES)


def _resident(block_shape, index_map):
    return pl.BlockSpec(block_shape, index_map, pipeline_mode=pl.Buffered(1))


def _rms_norm(x, gain):
    ms = jnp.mean(x * x, axis=-1, keepdims=True)
    return x * lax.rsqrt(ms + NORM_EPS) * gain


def _inproj_kernel(x_ref, gain_ref, w_ref, bias_ref, cos_ref, sina_ref, sinb_ref,
                   g0_ref, g1_ref, g2_ref, sb_ref, gate_ref):
    h = _rms_norm(x_ref[...], gain_ref[...]).astype(BF16)
    cos_t, sin_a, sin_b = cos_ref[...], sina_ref[...], sinb_ref[...]

    def rope(y):
        parts = []
        for s in range(y.shape[1] // LANES):
            ys = y[:, s * LANES:(s + 1) * LANES]
            parts.append(ys * cos_t
                         + pltpu.roll(ys, ROPE_HALF, 1) * sin_a
                         + pltpu.roll(ys, LANES - ROPE_HALF, 1) * sin_b)
        return jnp.concatenate(parts, axis=1)

    slab_refs = (g0_ref, g1_ref, g2_ref, sb_ref)
    for slab in range(N_SLABS):
        for part in range(3):
            c0 = slab * QKV_WIDTH + part * GROUP_WIDTH
            acc = jnp.dot(h, w_ref[:, c0:c0 + GROUP_WIDTH], preferred_element_type=F32)
            if slab < N_GROUPS and part < 2:
                acc = rope(acc)
            slab_refs[slab][:, part * GROUP_WIDTH:(part + 1) * GROUP_WIDTH] = acc.astype(BF16)
    gate0 = N_SLABS * QKV_WIDTH
    for c in range(gate_ref.shape[1] // COL_CHUNK):
        cs = slice(c * COL_CHUNK, (c + 1) * COL_CHUNK)
        acc = jnp.dot(h, w_ref[:, gate0 + c * COL_CHUNK:gate0 + (c + 1) * COL_CHUNK],
                      preferred_element_type=F32)
        gate_ref[:, cs] = jax.nn.sigmoid(acc + bias_ref[:, cs]).astype(BF16)


def _inproj(x, gain, w_in, bias, tables, layer):
    tok, d_model = x.shape
    n_cols = w_in.shape[2]
    gate_cols = n_cols - N_SLABS * QKV_WIDTH
    tm = TOKEN_TILE
    row = lambda i: (i, 0)
    const = lambda i: (0, 0)
    table_spec = pl.BlockSpec((tm, LANES), row)
    slab_shape = jax.ShapeDtypeStruct((tok, QKV_WIDTH), BF16)
    slab_spec = pl.BlockSpec((tm, QKV_WIDTH), row)
    return pl.pallas_call(
        _inproj_kernel,
        grid=(tok // tm,),
        in_specs=[
            pl.BlockSpec((tm, d_model), row),
            pl.BlockSpec((None, 1, d_model), lambda i: (layer, 0, 0)),
            _resident((None, d_model, n_cols), lambda i: (layer, 0, 0)),
            pl.BlockSpec((None, 1, gate_cols), lambda i: (layer, 0, 0)),
            table_spec, table_spec, table_spec,
        ],
        out_specs=[slab_spec] * N_SLABS + [pl.BlockSpec((tm, gate_cols), row)],
        out_shape=[slab_shape] * N_SLABS + [jax.ShapeDtypeStruct((tok, gate_cols), BF16)],
        compiler_params=_params(1),
        name="inproj",
    )(x, gain, w_in, bias, *tables)


def _dilated_kernel(x_ref, o_ref, lse_ref, *, dilation, n_blocks):
    blk = WINDOW_KEYS
    row = lax.broadcasted_iota(jnp.int32, (blk, 2 * blk), 0)
    col = lax.broadcasted_iota(jnp.int32, (blk, 2 * blk), 1)
    valid_two = (col >= row) & (col <= row + blk)
    valid_one = (lax.broadcasted_iota(jnp.int32, (blk, blk), 1)
                 <= lax.broadcasted_iota(jnp.int32, (blk, blk), 0))

    def attend(q, k, v, valid):
        s = lax.dot_general(q * SCALE, k, (((1,), (1,)), ((), ())), preferred_element_type=F32)
        s = jnp.where(valid, s, MASK_VALUE)
        m = jnp.max(s, axis=1, keepdims=True)
        p = jnp.exp(s - m)
        den = jnp.sum(p, axis=1, keepdims=True)
        pv = jnp.dot(p.astype(BF16), v, preferred_element_type=F32)
        return pv / den, m + jnp.log(den)

    def unit(r, h, q_rows, kv_rows, valid):
        base = r * QKV_WIDTH + h * HEAD_DIM
        q = x_ref[0, q_rows, base:base + HEAD_DIM]
        k = x_ref[0, kv_rows, base + GROUP_WIDTH:base + GROUP_WIDTH + HEAD_DIM]
        v = x_ref[0, kv_rows, base + 2 * GROUP_WIDTH:base + 2 * GROUP_WIDTH + HEAD_DIM]
        out, lse = attend(q, k, v, valid)
        o0 = r * GROUP_WIDTH + h * HEAD_DIM
        o_ref[0, q_rows, o0:o0 + HEAD_DIM] = out
        lse_ref[0, q_rows, o0:o0 + HEAD_DIM] = jnp.broadcast_to(lse, (blk, HEAD_DIM))

    for r in range(dilation):
        for h in range(HEADS):
            unit(r, h, slice(0, blk), slice(0, blk), valid_one)
        if n_blocks > 1:
            def body(i, carry, r=r):
                start = pl.multiple_of(i * blk, blk)
                prev = pl.multiple_of(i * blk - blk, blk)
                for h in range(HEADS):
                    unit(r, h, pl.ds(start, blk), pl.ds(prev, 2 * blk), valid_two)
                return carry
            lax.fori_loop(1, n_blocks, body, 0)


def _dilated(slab, batch, seq, dilation):
    rows = seq // dilation
    x = slab.reshape(batch, rows, dilation * QKV_WIDTH)
    out_shape = jax.ShapeDtypeStruct((batch, rows, dilation * GROUP_WIDTH), F32)
    out_spec = pl.BlockSpec((1, rows, dilation * GROUP_WIDTH), lambda b: (b, 0, 0))
    o, lse = pl.pallas_call(
        functools.partial(_dilated_kernel, dilation=dilation, n_blocks=rows // WINDOW_KEYS),
        grid=(batch,),
        in_specs=[pl.BlockSpec((1, rows, dilation * QKV_WIDTH), lambda b: (b, 0, 0))],
        out_specs=[out_spec, out_spec],
        out_shape=[out_shape, out_shape],
        compiler_params=_params(1),
        name=f"dilated_attention_d{dilation}",
    )(x)
    return o.reshape(batch * seq, GROUP_WIDTH), lse.reshape(batch * seq, GROUP_WIDTH)


def _stickbreak_kernel(q_ref, k_ref, v_ref, tri_ref, o_ref, acc_ref, run_ref):
    i = pl.program_id(1)
    blk = SB_BLOCK
    row = lax.broadcasted_iota(jnp.int32, (blk, blk), 0)
    col = lax.broadcasted_iota(jnp.int32, (blk, blk), 1)
    causal = col < row
    tri = tri_ref[...]

    for h in range(HEADS):
        hs = slice(h * HEAD_DIM, (h + 1) * HEAD_DIM)
        acc_ref[...] = jnp.zeros_like(acc_ref)
        run_ref[...] = jnp.zeros_like(run_ref)

        def process(j, masked, hs=hs):
            start = pl.multiple_of(j * blk, blk)
            q = q_ref[0, :, hs] * SCALE
            k = k_ref[0, pl.ds(start, blk), hs]
            v = v_ref[0, pl.ds(start, blk), hs]
            z = lax.dot_general(q, k, (((1,), (1,)), ((), ())), preferred_element_type=F32)
            softplus = jnp.maximum(z, 0.0) + jnp.log(1.0 + jnp.exp(-jnp.abs(z)))
            if masked:
                softplus = jnp.where(causal, softplus, 0.0)
            hi = softplus.astype(BF16)
            lo = (softplus - hi.astype(F32)).astype(BF16)
            suffix = (jnp.dot(hi, tri, preferred_element_type=F32)
                      + jnp.dot(lo, tri, preferred_element_type=F32))
            attn = jnp.exp(z - softplus - suffix - run_ref[...])
            if masked:
                attn = jnp.where(causal, attn, 0.0)
            acc_ref[...] += jnp.dot(attn.astype(BF16), v, preferred_element_type=F32)
            run_ref[...] += jnp.sum(softplus, axis=1, keepdims=True)

        process(i, True)

        def body(jj, carry):
            process(i - 1 - jj, False)
            return carry
        lax.fori_loop(0, i, body, 0)
        o_ref[0, :, hs] = acc_ref[...].astype(o_ref.dtype)


def _stickbreak(slab, tri, batch, seq):
    blk = SB_BLOCK
    x = slab.reshape(batch, seq, QKV_WIDTH)
    out = pl.pallas_call(
        _stickbreak_kernel,
        grid=(batch, seq // blk),
        in_specs=[
            pl.BlockSpec((1, blk, GROUP_WIDTH), lambda b, i: (b, i, 0)),
            pl.BlockSpec((1, seq, GROUP_WIDTH), lambda b, i: (b, 0, 1)),
            pl.BlockSpec((1, seq, GROUP_WIDTH), lambda b, i: (b, 0, 2)),
            _resident((blk, blk), lambda b, i: (0, 0)),
        ],
        out_specs=pl.BlockSpec((1, blk, GROUP_WIDTH), lambda b, i: (b, i, 0)),
        out_shape=jax.ShapeDtypeStruct((batch, seq, GROUP_WIDTH), BF16),
        scratch_shapes=[pltpu.VMEM((blk, HEAD_DIM), F32), pltpu.VMEM((blk, 1), F32)],
        compiler_params=_params(2),
        name="stickbreak_attention",
    )(x, x, x, tri)
    return out.reshape(batch * seq, GROUP_WIDTH)


def _outproj_kernel(x_ref, o0_ref, o1_ref, o2_ref, l0_ref, l1_ref, l2_ref, sb_ref,
                    ga_ref, gb_ref, wa_ref, wb_ref, wo_ref, out_ref):
    l0, l1, l2 = l0_ref[...], l1_ref[...], l2_ref[...]
    m = jnp.maximum(l0, jnp.maximum(l1, l2))
    e0, e1, e2 = jnp.exp(l0 - m), jnp.exp(l1 - m), jnp.exp(l2 - m)
    merged = (e0 * o0_ref[...] + e1 * o1_ref[...] + e2 * o2_ref[...]) / (e0 + e1 + e2)
    y_a = jnp.dot(merged.astype(BF16), wa_ref[...], preferred_element_type=F32)
    y_b = jnp.dot(sb_ref[...], wb_ref[...], preferred_element_type=F32)
    mixed = ga_ref[...].astype(F32) * y_a + gb_ref[...].astype(F32) * y_b
    out_ref[...] = x_ref[...] + jnp.dot(mixed.astype(BF16), wo_ref[...], preferred_element_type=F32)


def _outproj(x, outs, lses, sb, gates, w_a, w_b, w_o, layer):
    tok, d_model = x.shape
    tm = TOKEN_TILE
    row = lambda i: (i, 0)
    wsel = lambda i: (layer, 0, 0)
    head_spec = pl.BlockSpec((tm, GROUP_WIDTH), row)
    return pl.pallas_call(
        _outproj_kernel,
        grid=(tok // tm,),
        in_specs=[pl.BlockSpec((tm, d_model), row)] + [head_spec] * 7 + [
            pl.BlockSpec((tm, d_model), lambda i: (i, 0)),
            pl.BlockSpec((tm, d_model), lambda i: (i, 1)),
            _resident((None, GROUP_WIDTH, d_model), wsel),
            _resident((None, GROUP_WIDTH, d_model), wsel),
            _resident((None, d_model, d_model), wsel),
        ],
        out_specs=pl.BlockSpec((tm, d_model), row),
        out_shape=jax.ShapeDtypeStruct((tok, d_model), F32),
        compiler_params=_params(1),
        name="outproj",
    )(x, *outs, *lses, sb, gates, gates, w_a, w_b, w_o)


def _ffn_kernel(x_ref, gain_ref, wup_ref, cw_ref, cb_ref, wdn_ref, fgain_ref, out_ref,
                tail_ref, act_ref, *, tiles_per_seq, final_norm):
    tm = x_ref.shape[0]
    d_ff = act_ref.shape[1]
    x = x_ref[...]
    h = _rms_norm(x, gain_ref[...]).astype(BF16)
    @pl.when(pl.program_id(0) % tiles_per_seq == 0)
    def _():
        tail_ref[...] = jnp.zeros_like(tail_ref)

    row = lax.broadcasted_iota(jnp.int32, (tm, COL_CHUNK), 0)
    for c in range(d_ff // COL_CHUNK):
        cs = slice(c * COL_CHUNK, (c + 1) * COL_CHUNK)
        a = jnp.dot(h, wup_ref[:, cs], preferred_element_type=F32)
        b = jnp.dot(h, wup_ref[:, d_ff + c * COL_CHUNK:d_ff + (c + 1) * COL_CHUNK],
                    preferred_element_type=F32)
        tail = tail_ref[:, cs]
        tail_ref[:, cs] = a[tm - SUBLANES:, :]
        back1 = jnp.where(row == 0, tail[SUBLANES - 1:SUBLANES, :], pltpu.roll(a, 1, 0))
        back2 = jnp.where(row == 0, tail[SUBLANES - 2:SUBLANES - 1, :],
                          jnp.where(row == 1, tail[SUBLANES - 1:SUBLANES, :], pltpu.roll(a, 2, 0)))
        conv = (cw_ref[0:1, cs] * back2 + cw_ref[1:2, cs] * back1 + cw_ref[2:3, cs] * a
                + cb_ref[:, cs])
        act_ref[:, cs] = (conv * jax.nn.sigmoid(conv) * b).astype(BF16)
    y = x + jnp.dot(act_ref[...], wdn_ref[...], preferred_element_type=F32)
    if final_norm:
        y = _rms_norm(y, fgain_ref[...])
    out_ref[...] = y


def _ffn(x, gain, w_up, conv_w, conv_b, w_down, final_gain, layer, seq, final_norm):
    tok, d_model = x.shape
    d_ff = w_down.shape[1]
    tm = TOKEN_TILE
    row = lambda i: (i, 0)
    wsel = lambda i: (layer, 0, 0)
    return pl.pallas_call(
        functools.partial(_ffn_kernel, tiles_per_seq=seq // tm, final_norm=final_norm),
        grid=(tok // tm,),
        in_specs=[
            pl.BlockSpec((tm, d_model), row),
            pl.BlockSpec((None, 1, d_model), wsel),
            _resident((None, d_model, 2 * d_ff), wsel),
            pl.BlockSpec((None, CONV_WIDTH, d_ff), wsel),
            pl.BlockSpec((None, 1, d_ff), wsel),
            _resident((None, d_ff, d_model), wsel),
            pl.BlockSpec((1, d_model), lambda i: (0, 0)),
        ],
        out_specs=pl.BlockSpec((tm, d_model), row),
        out_shape=jax.ShapeDtypeStruct((tok, d_model), F32),
        scratch_shapes=[pltpu.VMEM((SUBLANES, d_ff), F32), pltpu.VMEM((tm, d_ff), BF16)],
        compiler_params=_params(1),
        name="conv_ffn",
    )(x, gain, w_up, conv_w, conv_b, w_down, final_gain)


def _rope_tables(positions):
    inv_freq = ROPE_THETA ** (-jnp.arange(0, ROPE_DIM, 2, dtype=F32) / ROPE_DIM)
    ang = positions.astype(F32)[..., None] * inv_freq
    cos, sin = jnp.cos(ang), jnp.sin(ang)
    pos_in_head = jnp.arange(LANES) % HEAD_DIM
    freq = pos_in_head % ROPE_HALF
    cos_l, sin_l = cos[..., freq], sin[..., freq]
    first_half = pos_in_head < ROPE_HALF
    second_half = (pos_in_head >= ROPE_HALF) & (pos_in_head < ROPE_DIM)
    tables = (jnp.where(pos_in_head < ROPE_DIM, cos_l, 1.0),
              jnp.where(second_half, sin_l, 0.0),
              jnp.where(first_half, -sin_l, 0.0))
    return tuple(t.reshape(-1, LANES) for t in tables)


def _slab_column_order():
    dil_width = N_GROUPS * GROUP_WIDTH
    cols = []
    for g in range(N_GROUPS):
        for part in range(3):
            start = part * dil_width + g * GROUP_WIDTH
            cols.extend(range(start, start + GROUP_WIDTH))
    return cols


def kernel(x, positions, norm_mix, w_in, b_gate, w_proj_a, w_proj_b, w_out, norm_ffn, w_up,
           conv_w, conv_b, w_down, norm_final):
    batch, seq, d_model = x.shape
    depth = w_in.shape[0]
    assert seq % (max(DILATIONS) * WINDOW_KEYS) == 0 and seq % TOKEN_TILE == 0
    assert seq % SB_BLOCK == 0 and (batch * seq) % TOKEN_TILE == 0

    n_slab_cols = 3 * N_GROUPS * GROUP_WIDTH
    order = jnp.asarray(_slab_column_order() + list(range(n_slab_cols, w_in.shape[2])))
    w_in_b = w_in[:, :, order].astype(BF16)
    w_a_b, w_b_b, w_o_b = w_proj_a.astype(BF16), w_proj_b.astype(BF16), w_out.astype(BF16)
    w_up_b, w_dn_b = w_up.astype(BF16), w_down.astype(BF16)
    tables = _rope_tables(positions)
    idx = jnp.arange(SB_BLOCK)
    tri = (idx[:, None] > idx[None, :]).astype(BF16)

    gain_mix = norm_mix[:, None, :]
    gain_ffn = norm_ffn[:, None, :]
    bias = b_gate[:, None, :]
    cbias = conv_b[:, None, :]
    fgain = norm_final[None, :]

    xf = x.reshape(batch * seq, d_model)
    for layer in range(depth):
        g0, g1, g2, sbs, gates = _inproj(xf, gain_mix, w_in_b, bias, tables, layer)
        outs, lses = [], []
        for slab, dilation in zip((g0, g1, g2), DILATIONS):
            o, lse = _dilated(slab, batch, seq, dilation)
            outs.append(o)
            lses.append(lse)
        sb = _stickbreak(sbs, tri, batch, seq)
        xf = _outproj(xf, outs, lses, sb, gates, w_a_b, w_b_b, w_o_b, layer)
        xf = _ffn(xf, gain_ffn, w_up_b, conv_w, cbias, w_dn_b, fgain, layer, seq,
                  final_norm=(layer == depth - 1))
    return xf.reshape(batch, seq, d_model)
```

```python
import functools

import jax
import jax.numpy as jnp
from jax import lax
from jax.experimental import pallas as pl
from jax.experimental.pallas import tpu as pltpu

F32 = jnp.float32
BF16 = jnp.bfloat16

HEAD_DIM = 64
HEADS = 4
GROUP_WIDTH = HEADS * HEAD_DIM
QKV_WIDTH = 3 * GROUP_WIDTH
DILATIONS = (1, 4, 16)
WINDOW_KEYS = 128
N_GROUPS = len(DILATIONS)
N_SLABS = N_GROUPS + 1
ROPE_DIM = HEAD_DIM // 4
ROPE_HALF = ROPE_DIM // 2
ROPE_THETA = 500000.0
NORM_EPS = 1e-5
MASK_VALUE = -1e30
CONV_WIDTH = 3
LOG2_E = 1.4426950408889634
QUERY_SCALE = HEAD_DIM ** -0.5 * LOG2_E

LANES = 128
SUBLANES = 8
COL_CHUNK = 256
VMEM_LIMIT_BYTES = 60 * 1024 * 1024
HEAD_PAIRS = GROUP_WIDTH // LANES
DEINTERLEAVE_STRIDE = 4

TOKEN_TILE = 1024
SB_Q_BLOCK = 256
SB_K_BLOCK = 256
SB_UNDERFLOW_BITS = 150.0
DIL_UNITS_PER_STEP = 4
DIL_UNITS_PER_STAGE = 2
DIL_MERGE_ROWS = 256


def _params(n_axes):
    return pltpu.CompilerParams(
        dimension_semantics=("arbitrary",) * n_axes,
        vmem_limit_bytes=VMEM_LIMIT_BYTES)


def _resident(block_shape, index_map):
    return pl.BlockSpec(block_shape, index_map, pipeline_mode=pl.Buffered(1))


def _rms_norm(x, gain):
    ms = jnp.mean(x * x, axis=-1, keepdims=True)
    return x * lax.rsqrt(ms + NORM_EPS) * gain


def _strided_rows(start, size, stride):
    return pl.ds(start, size) if stride == 1 else pl.ds(start, size, stride=stride)


def _inproj_kernel(x_ref, gain_ref, w_ref, bias_ref, cos_ref, sina_ref, sinb_ref,
                   g0_ref, g1_ref, g2_ref, sb_ref, gate_ref, stage_ref, stage2_ref):
    tm = x_ref.shape[0]
    h = _rms_norm(x_ref[...], gain_ref[...]).astype(BF16)
    cos_t, sin_a, sin_b = cos_ref[...], sina_ref[...], sinb_ref[...]

    def rope(ys):
        return (ys * cos_t + pltpu.roll(ys, ROPE_HALF, 1) * sin_a
                + pltpu.roll(ys, LANES - ROPE_HALF, 1) * sin_b)

    group_refs = (g0_ref, g1_ref, g2_ref)
    for slab in range(N_SLABS):
        for part in range(3):
            if slab < N_GROUPS:
                c0 = (part * N_GROUPS + slab) * GROUP_WIDTH
            else:
                c0 = (3 * N_GROUPS + part) * GROUP_WIDTH
            acc = jnp.dot(h, w_ref[:, c0:c0 + GROUP_WIDTH], preferred_element_type=F32)
            for s in range(HEAD_PAIRS):
                ys = acc[:, s * LANES:(s + 1) * LANES]
                if slab < N_GROUPS and part < 2:
                    ys = rope(ys)
                if part == 0:
                    ys = ys * QUERY_SCALE
                cols = slice(part * GROUP_WIDTH + s * LANES, part * GROUP_WIDTH + (s + 1) * LANES)
                dilation = DILATIONS[slab] if slab < N_GROUPS else 1
                if slab == N_GROUPS:
                    sb_ref[:, cols] = ys.astype(BF16)
                elif dilation == 1:
                    group_refs[slab][:, cols] = ys.astype(BF16)
                else:
                    stage_ref[s] = ys
                    if dilation <= DEINTERLEAVE_STRIDE:
                        for r in range(dilation):
                            rows = stage_ref[s, pl.ds(r, tm // dilation, stride=dilation), :]
                            group_refs[slab][0, r, :, cols] = rows.astype(BF16)
                    else:
                        d1, d2 = DEINTERLEAVE_STRIDE, dilation // DEINTERLEAVE_STRIDE
                        for r1 in range(d1):
                            stage2_ref[s, r1] = stage_ref[s, pl.ds(r1, tm // d1, stride=d1), :]
                        for r1 in range(d1):
                            for r2 in range(d2):
                                rows = stage2_ref[s, r1, pl.ds(r2, tm // dilation, stride=d2), :]
                                group_refs[slab][0, d1 * r2 + r1, :, cols] = rows.astype(BF16)
    gate0 = N_SLABS * QKV_WIDTH
    for c in range(gate_ref.shape[1] // COL_CHUNK):
        cs = slice(c * COL_CHUNK, (c + 1) * COL_CHUNK)
        acc = jnp.dot(h, w_ref[:, gate0 + c * COL_CHUNK:gate0 + (c + 1) * COL_CHUNK],
                      preferred_element_type=F32)
        gate_ref[:, cs] = jax.nn.sigmoid(acc + bias_ref[:, cs]).astype(BF16)


def _inproj(x, gain, w_in, bias, tables, layer, batch, seq):
    tok, d_model = x.shape
    n_cols = w_in.shape[2]
    gate_cols = n_cols - N_SLABS * QKV_WIDTH
    tm = TOKEN_TILE
    tiles_per_seq = seq // tm
    row = lambda i: (i, 0)
    table_spec = pl.BlockSpec((tm, LANES), row)
    flat_shape = jax.ShapeDtypeStruct((tok, QKV_WIDTH), BF16)
    flat_spec = pl.BlockSpec((tm, QKV_WIDTH), row)
    out_shapes, out_specs = [], []
    for dilation in DILATIONS:
        if dilation == 1:
            out_shapes.append(flat_shape)
            out_specs.append(flat_spec)
        else:
            out_shapes.append(jax.ShapeDtypeStruct((batch, dilation, seq // dilation, QKV_WIDTH), BF16))
            out_specs.append(pl.BlockSpec(
                (1, dilation, tm // dilation, QKV_WIDTH),
                lambda i: (i // tiles_per_seq, 0, i % tiles_per_seq, 0)))
    out_shapes += [flat_shape, jax.ShapeDtypeStruct((tok, gate_cols), BF16)]
    out_specs += [flat_spec, pl.BlockSpec((tm, gate_cols), row)]
    return pl.pallas_call(
        _inproj_kernel,
        grid=(tok // tm,),
        in_specs=[
            pl.BlockSpec((tm, d_model), row),
            pl.BlockSpec((None, 1, d_model), lambda i: (layer, 0, 0)),
            _resident((None, d_model, n_cols), lambda i: (layer, 0, 0)),
            pl.BlockSpec((None, 1, gate_cols), lambda i: (layer, 0, 0)),
            table_spec, table_spec, table_spec,
        ],
        out_specs=out_specs,
        out_shape=out_shapes,
        scratch_shapes=[
            pltpu.VMEM((HEAD_PAIRS, tm, LANES), F32),
            pltpu.VMEM((HEAD_PAIRS, DEINTERLEAVE_STRIDE, tm // DEINTERLEAVE_STRIDE, LANES), F32)],
        compiler_params=_params(1),
        name="inproj",
    )(x, gain, w_in, bias, *tables)


def _dilated_group(x_ref, o_ref, lse_ref, bias_ref, dilation, n_blocks):
    blk = WINDOW_KEYS
    lane = lax.broadcasted_iota(jnp.int32, (blk, LANES), 1)
    first_head = lane < HEAD_DIM

    def attend(units):
        per_stage = DIL_UNITS_PER_STEP if n_blocks == 1 else DIL_UNITS_PER_STAGE
        for g in range(0, len(units), per_stage):
            attend_staged(units[g:g + per_stage])

    def attend_staged(units):
        chains = []
        for r, i, kv_len in units:
            q_start = i * blk
            kv_start = q_start - (kv_len - blk)
            if not isinstance(i, int):
                q_start = pl.multiple_of(q_start, blk)
                kv_start = pl.multiple_of(kv_start, blk)
            for p in range(HEAD_PAIRS):
                chains.append((r, q_start, kv_start, kv_len, p))

        scores = []
        for r, q_start, kv_start, kv_len, p in chains:
            q = x_ref[0, r, pl.ds(q_start, blk), p * LANES:(p + 1) * LANES]
            zero = jnp.zeros_like(q)
            q2 = jnp.concatenate([jnp.where(first_head, q, zero), jnp.where(first_head, zero, q)],
                                 axis=0)
            k0 = GROUP_WIDTH + p * LANES
            k = x_ref[0, r, pl.ds(kv_start, kv_len), k0:k0 + LANES]
            scores.append(lax.dot_general(q2, k, (((1,), (1,)), ((), ())),
                                          preferred_element_type=F32))
        probs, maxes = [], []
        for (r, q_start, kv_start, kv_len, p), s in zip(chains, scores):
            s = s + (bias_ref[...] if kv_len == 2 * blk else bias_ref[:, blk:])
            m = jnp.max(s, axis=1, keepdims=True)
            probs.append(jnp.exp2(s - m).astype(BF16))
            maxes.append(m)
        weighted = []
        for (r, q_start, kv_start, kv_len, p), e in zip(chains, probs):
            v0 = 2 * GROUP_WIDTH + p * LANES
            v = x_ref[0, r, pl.ds(kv_start, kv_len), v0:v0 + LANES]
            ones = jnp.ones((kv_len, LANES), BF16)
            weighted.append(jnp.dot(e, jnp.concatenate([v, ones], axis=1),
                                    preferred_element_type=F32))
        for (r, q_start, kv_start, kv_len, p), pv, m in zip(chains, weighted, maxes):
            num = jnp.where(first_head, pv[:blk, :LANES], pv[blk:, :LANES])
            den = jnp.where(first_head, pv[:blk, LANES:], pv[blk:, LANES:])
            top = jnp.where(first_head, m[:blk], m[blk:])
            rows = _strided_rows(q_start * dilation + r, blk, dilation)
            o_ref[p, rows, :] = num / den
            lse_ref[p, rows, :] = top + jnp.log2(den)

    per_step = DIL_UNITS_PER_STEP
    if n_blocks == 1:
        def body(g, carry):
            attend([(g * per_step + u, 0, blk) for u in range(per_step)])
            return carry
        lax.fori_loop(0, dilation // per_step, body, 0)
    elif n_blocks <= per_step:
        def body(r, carry):
            attend([(r, 0, blk)] + [(r, i, 2 * blk) for i in range(1, n_blocks)])
            return carry
        lax.fori_loop(0, dilation, body, 0)
    else:
        assert dilation == 1 and n_blocks % per_step == 0
        attend([(0, 0, blk)] + [(0, i, 2 * blk) for i in range(1, per_step)])

        def body(g, carry):
            attend([(0, g * per_step + u, 2 * blk) for u in range(per_step)])
            return carry
        lax.fori_loop(1, n_blocks // per_step, body, 0)


def _dilated_kernel(x0_ref, x1_ref, x2_ref, out_ref, o_ref, lse_ref, bias_ref):
    blk = WINDOW_KEYS
    seq = out_ref.shape[1]
    qpos = lax.broadcasted_iota(jnp.int32, (2 * blk, 2 * blk), 0) & (blk - 1)
    kpos = lax.broadcasted_iota(jnp.int32, (2 * blk, 2 * blk), 1) - blk
    bias_ref[...] = jnp.where((kpos <= qpos) & (kpos >= qpos - blk), 0.0, MASK_VALUE)

    for g, (x_ref, dilation) in enumerate(zip((x0_ref, x1_ref, x2_ref), DILATIONS)):
        _dilated_group(x_ref, o_ref.at[g], lse_ref.at[g], bias_ref, dilation,
                       seq // dilation // blk)

    def merge(c, carry):
        rows = pl.ds(pl.multiple_of(c * DIL_MERGE_ROWS, DIL_MERGE_ROWS), DIL_MERGE_ROWS)
        for p in range(HEAD_PAIRS):
            lses = [lse_ref[g, p, rows, :] for g in range(N_GROUPS)]
            top = functools.reduce(jnp.maximum, lses)
            weights = [jnp.exp2(lse - top) for lse in lses]
            num = sum(w * o_ref[g, p, rows, :] for g, w in enumerate(weights))
            out_ref[0, rows, p * LANES:(p + 1) * LANES] = (num / sum(weights)).astype(out_ref.dtype)
        return carry
    lax.fori_loop(0, seq // DIL_MERGE_ROWS, merge, 0)


def _dilated(slabs, batch, seq):
    views, specs = [], []
    for slab, dilation in zip(slabs, DILATIONS):
        rows = seq // dilation
        views.append(slab.reshape(batch, dilation, rows, QKV_WIDTH))
        specs.append(pl.BlockSpec((1, dilation, rows, QKV_WIDTH), lambda b: (b, 0, 0, 0)))
    out = pl.pallas_call(
        _dilated_kernel,
        grid=(batch,),
        in_specs=specs,
        out_specs=pl.BlockSpec((1, seq, GROUP_WIDTH), lambda b: (b, 0, 0)),
        out_shape=jax.ShapeDtypeStruct((batch, seq, GROUP_WIDTH), BF16),
        scratch_shapes=[pltpu.VMEM((N_GROUPS, HEAD_PAIRS, seq, LANES), F32),
                        pltpu.VMEM((N_GROUPS, HEAD_PAIRS, seq, LANES), F32),
                        pltpu.VMEM((2 * WINDOW_KEYS, 2 * WINDOW_KEYS), F32)],
        compiler_params=_params(1),
        name="dilated_attention",
    )(*views)
    return out.reshape(batch * seq, GROUP_WIDTH)


def _stickbreak_kernel(q_ref, k_ref, v_ref, tri_ref, o_ref, acc_ref, run_ref, qs_ref):
    i = pl.program_id(1)
    tq, tk = SB_Q_BLOCK, SB_K_BLOCK
    lane = lax.broadcasted_iota(jnp.int32, (tk, LANES), 1)
    first_head = lane < HEAD_DIM
    tri = tri_ref[...]
    chunk = 2 * tk
    n_seg = tq // tk

    def prepare():
        acc_ref[...] = jnp.zeros_like(acc_ref)
        run_ref[...] = jnp.zeros_like(run_ref)
        for p in range(HEAD_PAIRS):
            for seg in range(n_seg):
                q = q_ref[0, seg * tk:(seg + 1) * tk, p * LANES:(p + 1) * LANES]
                zero = jnp.zeros_like(q)
                qs_ref[p, seg * chunk:seg * chunk + tk, :] = jnp.where(first_head, q, zero)
                qs_ref[p, seg * chunk + tk:(seg + 1) * chunk, :] = jnp.where(first_head, zero, q)

    def process(j, key_offset):
        keys = pl.ds(pl.multiple_of(j * tk, tk), tk)
        chains = []
        for p in range(HEAD_PAIRS):
            for seg in range(n_seg):
                q_first = seg * tk
                if key_offset is not None and q_first + tk - 1 <= key_offset:
                    continue
                causal = None
                if key_offset is not None and q_first <= key_offset + tk - 1:
                    qpos = (lax.broadcasted_iota(jnp.int32, (chunk, tk), 0) & (tk - 1)) + q_first
                    kpos = lax.broadcasted_iota(jnp.int32, (chunk, tk), 1) + key_offset
                    causal = kpos < qpos
                chains.append((p, slice(seg * chunk, (seg + 1) * chunk), causal))

        logits = [lax.dot_general(qs_ref[p, rs, :], k_ref[0, keys, p * LANES:(p + 1) * LANES],
                                  (((1,), (1,)), ((), ())), preferred_element_type=F32)
                  for p, rs, causal in chains]
        softplus = []
        for (p, rs, causal), z in zip(chains, logits):
            sp = jnp.maximum(z, 0.0) + jnp.log2(1.0 + jnp.exp2(-jnp.abs(z)))
            softplus.append(sp if causal is None else jnp.where(causal, sp, 0.0))
        suffix = [jnp.dot(sp.astype(BF16), tri, preferred_element_type=F32) for sp in softplus]
        weights = []
        for (p, rs, causal), z, sp, suf in zip(chains, logits, softplus, suffix):
            attn = jnp.exp2(z - sp - suf - run_ref[p, rs, :])
            if causal is not None:
                attn = jnp.where(causal, attn, 0.0)
            weights.append(attn.astype(BF16))
            run_ref[p, rs, :] += jnp.sum(sp, axis=1, keepdims=True)
        for (p, rs, causal), attn in zip(chains, weights):
            acc_ref[p, rs, :] += jnp.dot(attn, v_ref[0, keys, p * LANES:(p + 1) * LANES],
                                         preferred_element_type=F32)

    blocks_per_q = tq // tk
    first = i * blocks_per_q

    def overlapping():
        prepare()
        for d in reversed(range(blocks_per_q)):
            process(first + d, d * tk)

    @pl.when(i == 0)
    def _():
        overlapping()

    @pl.when(i > 0)
    def _():
        overlapping()
        process(first - 1, None)

    def more_to_add():
        return jnp.min(run_ref[...]) < SB_UNDERFLOW_BITS

    def cond(carry):
        jj, go = carry
        return jnp.logical_and(jj < first, go)

    def body(carry):
        jj, _ = carry
        process(first - 1 - jj, None)
        return jj + 1, more_to_add()
    lax.while_loop(cond, body, (jnp.int32(1), more_to_add()))

    for p in range(HEAD_PAIRS):
        for seg in range(n_seg):
            out = jnp.where(first_head, acc_ref[p, seg * chunk:seg * chunk + tk, :],
                            acc_ref[p, seg * chunk + tk:(seg + 1) * chunk, :])
            o_ref[0, seg * tk:(seg + 1) * tk, p * LANES:(p + 1) * LANES] = out.astype(o_ref.dtype)


def _stickbreak(slab, tri, batch, seq):
    tq, tk = SB_Q_BLOCK, SB_K_BLOCK
    x = slab.reshape(batch, seq, QKV_WIDTH)
    out = pl.pallas_call(
        _stickbreak_kernel,
        grid=(batch, seq // tq),
        in_specs=[
            pl.BlockSpec((1, tq, GROUP_WIDTH), lambda b, i: (b, i, 0)),
            pl.BlockSpec((1, seq, GROUP_WIDTH), lambda b, i: (b, 0, 1)),
            pl.BlockSpec((1, seq, GROUP_WIDTH), lambda b, i: (b, 0, 2)),
            _resident((tk, tk), lambda b, i: (0, 0)),
        ],
        out_specs=pl.BlockSpec((1, tq, GROUP_WIDTH), lambda b, i: (b, i, 0)),
        out_shape=jax.ShapeDtypeStruct((batch, seq, GROUP_WIDTH), BF16),
        scratch_shapes=[pltpu.VMEM((HEAD_PAIRS, 2 * tq, LANES), F32),
                        pltpu.VMEM((HEAD_PAIRS, 2 * tq, 1), F32),
                        pltpu.VMEM((HEAD_PAIRS, 2 * tq, LANES), BF16)],
        compiler_params=_params(2),
        name="stickbreak_attention",
    )(x, x, x, tri)
    return out.reshape(batch * seq, GROUP_WIDTH)


def _mix_ffn_kernel(x_ref, dil_ref, sb_ref, ga_ref, gb_ref, wa_ref, wb_ref, wo_ref, gain_ref,
                    wup_ref, cw_ref, cb_ref, wdn_ref, fgain_ref, out_ref, tail_ref, act_ref, *,
                    tiles_per_seq, final_norm):
    tm = x_ref.shape[0]
    d_ff = act_ref.shape[1]

    @pl.when(pl.program_id(0) % tiles_per_seq == 0)
    def _():
        tail_ref[...] = jnp.zeros_like(tail_ref)

    y_a = jnp.dot(dil_ref[...], wa_ref[...], preferred_element_type=F32)
    y_b = jnp.dot(sb_ref[...], wb_ref[...], preferred_element_type=F32)
    mixed = ga_ref[...].astype(F32) * y_a + gb_ref[...].astype(F32) * y_b
    x = x_ref[...] + jnp.dot(mixed.astype(BF16), wo_ref[...], preferred_element_type=F32)

    h = _rms_norm(x, gain_ref[...]).astype(BF16)
    row = lax.broadcasted_iota(jnp.int32, (tm, COL_CHUNK), 0)
    for c in range(d_ff // COL_CHUNK):
        cs = slice(c * COL_CHUNK, (c + 1) * COL_CHUNK)
        a = jnp.dot(h, wup_ref[:, cs], preferred_element_type=F32)
        b = jnp.dot(h, wup_ref[:, d_ff + c * COL_CHUNK:d_ff + (c + 1) * COL_CHUNK],
                    preferred_element_type=F32)
        tail = tail_ref[:, cs]
        tail_ref[:, cs] = a[tm - SUBLANES:, :]
        back1 = jnp.where(row == 0, tail[SUBLANES - 1:SUBLANES, :], pltpu.roll(a, 1, 0))
        back2 = jnp.where(row == 0, tail[SUBLANES - 2:SUBLANES - 1, :],
                          jnp.where(row == 1, tail[SUBLANES - 1:SUBLANES, :], pltpu.roll(a, 2, 0)))
        conv = (cw_ref[0:1, cs] * back2 + cw_ref[1:2, cs] * back1 + cw_ref[2:3, cs] * a
                + cb_ref[:, cs])
        act_ref[:, cs] = (conv * jax.nn.sigmoid(conv) * b).astype(BF16)
    y = x + jnp.dot(act_ref[...], wdn_ref[...], preferred_element_type=F32)
    if final_norm:
        y = _rms_norm(y, fgain_ref[...])
    out_ref[...] = y


def _mix_ffn(x, dil, sb, gates, w_a, w_b, w_o, gain, w_up, conv_w, conv_b, w_down,
             final_gain, layer, seq, final_norm):
    tok, d_model = x.shape
    d_ff = w_down.shape[1]
    tm = TOKEN_TILE
    tiles_per_seq = seq // tm
    row = lambda i: (i, 0)
    wsel = lambda i: (layer, 0, 0)
    return pl.pallas_call(
        functools.partial(_mix_ffn_kernel, tiles_per_seq=tiles_per_seq, final_norm=final_norm),
        grid=(tok // tm,),
        in_specs=[
            pl.BlockSpec((tm, d_model), row),
            pl.BlockSpec((tm, GROUP_WIDTH), row),
            pl.BlockSpec((tm, GROUP_WIDTH), row),
            pl.BlockSpec((tm, d_model), lambda i: (i, 0)),
            pl.BlockSpec((tm, d_model), lambda i: (i, 1)),
            _resident((None, GROUP_WIDTH, d_model), wsel),
            _resident((None, GROUP_WIDTH, d_model), wsel),
            _resident((None, d_model, d_model), wsel),
            pl.BlockSpec((None, 1, d_model), wsel),
            _resident((None, d_model, 2 * d_ff), wsel),
            pl.BlockSpec((None, CONV_WIDTH, d_ff), wsel),
            pl.BlockSpec((None, 1, d_ff), wsel),
            _resident((None, d_ff, d_model), wsel),
            pl.BlockSpec((1, d_model), lambda i: (0, 0)),
        ],
        out_specs=pl.BlockSpec((tm, d_model), row),
        out_shape=jax.ShapeDtypeStruct((tok, d_model), F32),
        scratch_shapes=[pltpu.VMEM((SUBLANES, d_ff), F32), pltpu.VMEM((tm, d_ff), BF16)],
        compiler_params=_params(1),
        name="mix_ffn",
    )(x, dil, sb, gates, gates, w_a, w_b, w_o, gain, w_up, conv_w, conv_b, w_down, final_gain)


def _rope_tables(positions):
    inv_freq = ROPE_THETA ** (-jnp.arange(0, ROPE_DIM, 2, dtype=F32) / ROPE_DIM)
    ang = positions.astype(F32)[..., None] * inv_freq
    cos, sin = jnp.cos(ang), jnp.sin(ang)
    pos_in_head = jnp.arange(LANES) % HEAD_DIM
    cos_l = jnp.tile(cos, (1, 1, LANES // ROPE_HALF))
    sin_l = jnp.tile(sin, (1, 1, LANES // ROPE_HALF))
    first_half = pos_in_head < ROPE_HALF
    second_half = (pos_in_head >= ROPE_HALF) & (pos_in_head < ROPE_DIM)
    tables = (jnp.where(pos_in_head < ROPE_DIM, cos_l, 1.0),
              jnp.where(second_half, sin_l, 0.0),
              jnp.where(first_half, -sin_l, 0.0))
    return tuple(t.reshape(-1, LANES) for t in tables)


def kernel(x, positions, norm_mix, w_in, b_gate, w_proj_a, w_proj_b, w_out, norm_ffn, w_up,
           conv_w, conv_b, w_down, norm_final):
    batch, seq, d_model = x.shape
    depth = w_in.shape[0]
    assert seq % (max(DILATIONS) * WINDOW_KEYS) == 0
    assert seq % TOKEN_TILE == 0 and TOKEN_TILE % (max(DILATIONS) * 2 * SUBLANES) == 0
    assert seq % SB_Q_BLOCK == 0 and SB_Q_BLOCK % SB_K_BLOCK == 0

    w_in_b = w_in.astype(BF16)
    w_a_b, w_b_b, w_o_b = w_proj_a.astype(BF16), w_proj_b.astype(BF16), w_out.astype(BF16)
    w_up_b, w_dn_b = w_up.astype(BF16), w_down.astype(BF16)
    tables = _rope_tables(positions)
    idx = jnp.arange(SB_K_BLOCK)
    tri = (idx[:, None] > idx[None, :]).astype(BF16)

    gain_mix = norm_mix[:, None, :]
    gain_ffn = norm_ffn[:, None, :]
    bias = b_gate[:, None, :]
    cbias = conv_b[:, None, :]
    fgain = norm_final[None, :]

    xf = x.reshape(batch * seq, d_model)
    for layer in range(depth):
        g0, g1, g2, sbs, gates = _inproj(xf, gain_mix, w_in_b, bias, tables, layer, batch, seq)
        dil = _dilated((g0, g1, g2), batch, seq)
        sb = _stickbreak(sbs, tri, batch, seq)
        xf = _mix_ffn(xf, dil, sb, gates, w_a_b, w_b_b, w_o_b, gain_ffn, w_up_b, conv_w,
                      cbias, w_dn_b, fgain, layer, seq, final_norm=(layer == depth - 1))
    return xf.reshape(batch, seq, d_model)
```
